```python
import math
import jax, jax.numpy as jnp
from jax import lax
import numpy as np

D_MODEL = 4096
BATCH = 2
SEQ = 4096
DEPTH = 2

D_MIX = D_MODEL
ATT_W = D_MIX // 4
SSM_W = D_MIX // 4
RWKV_W = D_MIX // 4
MLP_W = D_MIX // 4
ATT_HEAD_DIM = 128
ATT_HEADS = ATT_W // ATT_HEAD_DIM
IDX_HEADS = 16
IDX_DIM = 64
IDX_TOPK_MAX = 256
Q_BLOCK = 128
SSM_HEAD_DIM = 64
SSM_HEADS = SSM_W // SSM_HEAD_DIM
SSM_GROUPS = 4
SSM_STATE = 128
SSM_CONV = 4
SSM_CHUNK = 128
SSM_CONV_CH = SSM_W + 2 * SSM_GROUPS * SSM_STATE
RWKV_HEAD_DIM = 64
RWKV_HEADS = RWKV_W // RWKV_HEAD_DIM
RWKV_LORA_W = 64
RWKV_LORA_A = 64
MLP_GROUPS = 8
MLP_CHUNK = 128
NORM_EPS = 1e-5
GN_EPS = 64e-5

ATT_SPLITS = (ATT_W, ATT_W, ATT_W, ATT_W, IDX_HEADS * IDX_DIM, IDX_DIM, IDX_HEADS)
SSM_SPLITS = (SSM_W, SSM_CONV_CH, SSM_HEADS)
XBC_SPLITS = (SSM_W, SSM_GROUPS * SSM_STATE, SSM_GROUPS * SSM_STATE)
RWKV_SPLITS = (RWKV_W, RWKV_W, RWKV_W, RWKV_W, RWKV_LORA_W, RWKV_LORA_A)
MLP_SPLITS = (MLP_W, MLP_W, MLP_W)
ATT_PROJ = 4 * ATT_W + IDX_HEADS * IDX_DIM + IDX_DIM + IDX_HEADS
SSM_PROJ = SSM_W + SSM_CONV_CH + SSM_HEADS
RWKV_PROJ = 4 * RWKV_W + RWKV_LORA_W + RWKV_LORA_A
MLP_PROJ = 3 * MLP_W
D_IN = ATT_PROJ + SSM_PROJ + RWKV_PROJ + MLP_PROJ
BRANCH_SPLITS = (ATT_PROJ, SSM_PROJ, RWKV_PROJ, MLP_PROJ)

kernel_name = 'hymba_style_dsa_ssd_rwkv7_gmlp_trunk'


def split_cols(p, sizes):
    cuts = [int(c) for c in np.cumsum(sizes)[:-1]]
    return jnp.split(p, cuts, axis=-1)


def rms_norm(u, g):
    uf = u.astype(jnp.float32)
    y = uf * lax.rsqrt(jnp.mean(uf * uf, axis=-1, keepdims=True) + NORM_EPS)
    return (y * g.astype(jnp.float32)).astype(u.dtype)


def dsa_attention(q, k, v, q_idx, k_idx, w_idx):
    bsz, seq = q.shape[0], q.shape[1]
    top_k = min(IDX_TOPK_MAX, seq // 4)
    key_pos = jnp.arange(seq)
    att_scale = ATT_HEAD_DIM ** -0.5
    idx_scale = IDX_DIM ** -0.5
    head_w_scale = IDX_HEADS ** -0.5
    gather = jax.vmap(lambda table, ids: table[ids])

    def one_block(blk):
        start = blk * Q_BLOCK
        qb = lax.dynamic_slice_in_dim(q, start, Q_BLOCK, axis=1)
        qib = lax.dynamic_slice_in_dim(q_idx, start, Q_BLOCK, axis=1)
        wib = lax.dynamic_slice_in_dim(w_idx, start, Q_BLOCK, axis=1).astype(jnp.float32)
        q_pos = start + jnp.arange(Q_BLOCK)
        causal = key_pos[None, :] <= q_pos[:, None]
        dots = jnp.einsum('bqhd,bsd->bqhs', qib, k_idx).astype(jnp.float32) * idx_scale
        score = jnp.einsum('bqh,bqhs->bqs', wib * head_w_scale, jax.nn.relu(dots))
        score = jnp.where(causal[None], score, -jnp.inf)
        _, sel = lax.top_k(score, top_k)
        k_sel = gather(k, sel)
        v_sel = gather(v, sel)
        logits = jnp.einsum('bqhd,bqjhd->bhqj', qb, k_sel).astype(jnp.float32) * att_scale
        valid = sel <= q_pos[None, :, None]
        logits = jnp.where(valid[:, None], logits, -jnp.inf)
        probs = jax.nn.softmax(logits, axis=-1).astype(v.dtype)
        return jnp.einsum('bhqj,bqjhd->bqhd', probs, v_sel)

    out = lax.map(one_block, jnp.arange(seq // Q_BLOCK))
    return jnp.swapaxes(out, 0, 1).reshape(bsz, seq, ATT_HEADS, ATT_HEAD_DIM)


def attention_branch(p):
    bsz, seq = p.shape[0], p.shape[1]
    q, k, v, z, qi, ki, wi = split_cols(p, ATT_SPLITS)
    heads = lambda t: t.reshape(bsz, seq, ATT_HEADS, ATT_HEAD_DIM)
    o = dsa_attention(heads(q), heads(k), heads(v), qi.reshape(bsz, seq, IDX_HEADS, IDX_DIM), ki, wi)
    return o.reshape(bsz, seq, ATT_W) * jax.nn.silu(z)


def causal_depthwise_conv(u, w, b):
    ch = u.shape[-1]
    out = lax.conv_general_dilated(u, w[:, None, :], window_strides=(1,), padding=[(SSM_CONV - 1, 0)],
                                   dimension_numbers=('NWC', 'WIO', 'NWC'), feature_group_count=ch)
    return out + b


def segsum(a):
    t = a.shape[-1]
    ar = jnp.broadcast_to(a[..., :, None], a.shape + (t,))
    strict = jnp.tril(jnp.ones((t, t), dtype=bool), -1)
    cs = jnp.cumsum(jnp.where(strict, ar, 0.0), axis=-2)
    return jnp.where(jnp.tril(jnp.ones((t, t), dtype=bool)), cs, -jnp.inf)


def ssd_scan(xh, dt, a_head, bg, cg):
    bsz, seq = xh.shape[0], xh.shape[1]
    nc = seq // SSM_CHUNK
    hpg = SSM_HEADS // SSM_GROUPS
    x = (xh * dt[..., None]).reshape(bsz, nc, SSM_CHUNK, SSM_HEADS, SSM_HEAD_DIM)
    a = jnp.transpose((dt * a_head).reshape(bsz, nc, SSM_CHUNK, SSM_HEADS), (0, 3, 1, 2))
    bc = bg.reshape(bsz, nc, SSM_CHUNK, SSM_GROUPS, SSM_STATE)
    cc = cg.reshape(bsz, nc, SSM_CHUNK, SSM_GROUPS, SSM_STATE)
    a_cum = jnp.cumsum(a, axis=-1)
    decay_in = jnp.exp(segsum(a))
    cb = jnp.repeat(jnp.einsum('bclgn,bcsgn->bgcls', cc, bc), hpg, axis=1)
    y_diag = jnp.einsum('bhcls,bcshp->bclhp', cb * decay_in, x)
    bh = jnp.repeat(bc, hpg, axis=3)
    ch = jnp.repeat(cc, hpg, axis=3)
    decay_states = jnp.exp(a_cum[..., -1:] - a_cum)
    states = jnp.einsum('bclhn,bhcl,bclhp->bchpn', bh, decay_states, x)
    chunk_decay = jnp.exp(a_cum[..., -1])

    def step(h, inp):
        s_c, d_c = inp
        return h * d_c[..., None, None] + s_c, h

    h0 = jnp.zeros((bsz, SSM_HEADS, SSM_HEAD_DIM, SSM_STATE), xh.dtype)
    _, init_states = lax.scan(step, h0, (jnp.transpose(states, (1, 0, 2, 3, 4)), jnp.transpose(chunk_decay, (2, 0, 1))))
    init_states = jnp.transpose(init_states, (1, 0, 2, 3, 4))
    y_off = jnp.einsum('bclhn,bchpn,bhcl->bclhp', ch, init_states, jnp.exp(a_cum))
    return (y_diag + y_off).reshape(bsz, seq, SSM_HEADS, SSM_HEAD_DIM)


def ssm_branch(p, conv_w, conv_b, dt_bias, a_log, d_skip, norm_g):
    bsz, seq = p.shape[0], p.shape[1]
    z, xbc, dt_raw = split_cols(p, SSM_SPLITS)
    xbc = jax.nn.silu(causal_depthwise_conv(xbc, conv_w, conv_b)).astype(jnp.float32)
    xs, bs, cs = split_cols(xbc, XBC_SPLITS)
    dt = jax.nn.softplus(dt_raw.astype(jnp.float32) + dt_bias.astype(jnp.float32))
    a_head = -jnp.exp(a_log.astype(jnp.float32))
    xh = xs.reshape(bsz, seq, SSM_HEADS, SSM_HEAD_DIM)
    y = ssd_scan(xh, dt, a_head,
                 bs.reshape(bsz, seq, SSM_GROUPS, SSM_STATE), cs.reshape(bsz, seq, SSM_GROUPS, SSM_STATE))
    y = y + d_skip.astype(jnp.float32)[:, None] * xh
    y = y.reshape(bsz, seq, SSM_W) * jax.nn.silu(z.astype(jnp.float32))
    yg = y.reshape(bsz, seq, SSM_GROUPS, SSM_W // SSM_GROUPS)
    yg = yg * lax.rsqrt(jnp.mean(yg * yg, axis=-1, keepdims=True) + NORM_EPS)
    return (yg.reshape(bsz, seq, SSM_W) * norm_g.astype(jnp.float32)).astype(p.dtype)


def rwkv7_scan(r, w, k, v, a, b):
    bsz, seq = r.shape[0], r.shape[1]
    tm = lambda t: jnp.moveaxis(t, 1, 0)

    def step(st, inp):
        r_t, w_t, k_t, v_t, a_t, b_t = inp
        sa = jnp.einsum('bhij,bhj->bhi', st, a_t)
        st = st * w_t[:, :, None, :] + sa[..., None] * b_t[:, :, None, :] + v_t[..., None] * k_t[:, :, None, :]
        return st, jnp.einsum('bhij,bhj->bhi', st, r_t)

    s0 = jnp.zeros((bsz, RWKV_HEADS, RWKV_HEAD_DIM, RWKV_HEAD_DIM), r.dtype)
    _, ys = lax.scan(step, s0, (tm(r), tm(w), tm(k), tm(v), tm(a), tm(b)))
    return jnp.moveaxis(ys, 0, 1)


def rwkv_branch(p, mu, w0, w2, a0, a2, k_k, k_a, r_k, gn_g, gn_b):
    bsz, seq = p.shape[0], p.shape[1]
    pf = p.astype(jnp.float32)
    prev = jnp.pad(pf[:, :-1], ((0, 0), (1, 0), (0, 0)))
    pf = pf + (prev - pf) * mu.astype(jnp.float32)
    r, k, v, z, wl, al = split_cols(pf, RWKV_SPLITS)
    w_raw = -jax.nn.softplus(-(w0 + jnp.tanh(wl) @ w2)) - 0.5
    decay = jnp.exp(-jnp.exp(w_raw))
    a = jax.nn.sigmoid(a0 + al @ a2)
    kk = k * k_k
    k = k * (1.0 + (a - 1.0) * k_a)
    heads = lambda t: t.reshape(bsz, seq, RWKV_HEADS, RWKV_HEAD_DIM)
    r, k, v, kk, a, decay = heads(r), heads(k), heads(v), heads(kk), heads(a), heads(decay)
    kk = kk * lax.rsqrt(jnp.sum(kk * kk, axis=-1, keepdims=True) + 1e-12)
    y = rwkv7_scan(r, decay, k, v, -kk, kk * a)
    mean = jnp.mean(y, axis=-1, keepdims=True)
    var = jnp.mean(jnp.square(y - mean), axis=-1, keepdims=True)
    yn = ((y - mean) * lax.rsqrt(var + GN_EPS)).reshape(bsz, seq, RWKV_W) * gn_g + gn_b
    bonus = (jnp.sum(r * k * r_k, axis=-1, keepdims=True) * v).reshape(bsz, seq, RWKV_W)
    return ((yn + bonus) * jax.nn.silu(z)).astype(p.dtype)


def mlp_branch(p, ln_g, ln_b, w_s, b_s):
    bsz, seq = p.shape[0], p.shape[1]
    u, v, z = split_cols(p, MLP_SPLITS)
    vf = v.astype(jnp.float32)
    mean = jnp.mean(vf, axis=-1, keepdims=True)
    var = jnp.mean(jnp.square(vf - mean), axis=-1, keepdims=True)
    vn = (vf - mean) * lax.rsqrt(var + NORM_EPS) * ln_g + ln_b
    vc = vn.reshape(bsz, seq // MLP_CHUNK, MLP_CHUNK, MLP_GROUPS, MLP_W // MLP_GROUPS)
    w_causal = w_s * jnp.tril(jnp.ones((MLP_CHUNK, MLP_CHUNK), dtype=w_s.dtype))
    vm = jnp.einsum('gts,bcsgd->bctgd', w_causal, vc.astype(w_s.dtype)) + jnp.transpose(b_s)[:, :, None]
    return u * vm.reshape(bsz, seq, MLP_W).astype(u.dtype) * jax.nn.silu(z)


def hybrid_layer(x, norm_g, w_in, w_out, conv_w, conv_b, dt_bias, a_log, d_skip, ssm_norm_g,
                 mu, w0, w2, a0, a2, k_k, k_a, r_k, gn_g, gn_b, ln_g, ln_b, w_s, b_s):
    h = rms_norm(x, norm_g)
    proj = jnp.einsum('bsd,de->bse', h, w_in)
    p_att, p_ssm, p_rwkv, p_mlp = split_cols(proj, BRANCH_SPLITS)
    y = jnp.concatenate([
        attention_branch(p_att),
        ssm_branch(p_ssm, conv_w, conv_b, dt_bias, a_log, d_skip, ssm_norm_g),
        rwkv_branch(p_rwkv, mu, w0, w2, a0, a2, k_k, k_a, r_k, gn_g, gn_b),
        mlp_branch(p_mlp, ln_g, ln_b, w_s, b_s),
    ], axis=-1).astype(x.dtype)
    return x + jnp.einsum('bse,ed->bsd', y, w_out)


def setup_inputs(seed: int = 0) -> dict:
    key = jax.random.key(seed)
    ks = jax.random.split(key, 26)
    f32 = jnp.float32
    nrm = lambda k, shape, s: s * jax.random.normal(k, shape, f32)
    x = jax.random.normal(ks[0], (BATCH, SEQ, D_MODEL), f32)
    norm_g = 1.0 + nrm(ks[1], (DEPTH, D_MODEL), 0.1)
    w_in = nrm(ks[2], (DEPTH, D_MODEL, D_IN), D_MODEL ** -0.5)
    w_out = nrm(ks[3], (DEPTH, D_MIX, D_MODEL), D_MIX ** -0.5)
    ssm_conv_w = nrm(ks[4], (DEPTH, SSM_CONV, SSM_CONV_CH), SSM_CONV ** -0.5)
    ssm_conv_b = nrm(ks[5], (DEPTH, SSM_CONV_CH), 0.02)
    dt = jnp.exp(jax.random.uniform(ks[6], (DEPTH, SSM_HEADS), f32, math.log(1e-3), math.log(1e-1)))
    ssm_dt_bias = dt + jnp.log(-jnp.expm1(-dt))
    ssm_A_log = jnp.log(jax.random.uniform(ks[7], (DEPTH, SSM_HEADS), f32, 1.0, 16.0))
    ssm_D = 1.0 + nrm(ks[8], (DEPTH, SSM_HEADS), 0.1)
    ssm_norm_g = 1.0 + nrm(ks[9], (DEPTH, SSM_W), 0.1)
    rwkv_mu = jax.random.uniform(ks[10], (DEPTH, RWKV_PROJ), f32, 0.0, 1.0)
    rwkv_w0 = jax.random.uniform(ks[11], (DEPTH, RWKV_W), f32, -6.0, 1.0)
    rwkv_w2 = nrm(ks[12], (DEPTH, RWKV_LORA_W, RWKV_W), 0.1)
    rwkv_a0 = nrm(ks[13], (DEPTH, RWKV_W), 0.5)
    rwkv_a2 = nrm(ks[14], (DEPTH, RWKV_LORA_A, RWKV_W), 0.1)
    rwkv_k_k = 0.85 + nrm(ks[15], (DEPTH, RWKV_W), 0.05)
    rwkv_k_a = 1.0 + nrm(ks[16], (DEPTH, RWKV_W), 0.05)
    rwkv_r_k = nrm(ks[17], (DEPTH, RWKV_HEADS, RWKV_HEAD_DIM), 0.1)
    rwkv_gn_g = 1.0 + nrm(ks[18], (DEPTH, RWKV_W), 0.1)
    rwkv_gn_b = nrm(ks[19], (DEPTH, RWKV_W), 0.02)
    mlp_ln_g = 1.0 + nrm(ks[20], (DEPTH, MLP_W), 0.1)
    mlp_ln_b = nrm(ks[21], (DEPTH, MLP_W), 0.02)
    mlp_w_s = nrm(ks[22], (DEPTH, MLP_GROUPS, MLP_CHUNK, MLP_CHUNK), MLP_CHUNK ** -0.5)
    mlp_b_s = 1.0 + nrm(ks[23], (DEPTH, MLP_GROUPS, MLP_CHUNK), 0.1)
    final_norm_g = 1.0 + nrm(ks[24], (D_MODEL,), 0.1)
    return {'x': x, 'norm_g': norm_g, 'w_in': w_in, 'w_out': w_out,
            'ssm_conv_w': ssm_conv_w, 'ssm_conv_b': ssm_conv_b, 'ssm_dt_bias': ssm_dt_bias,
            'ssm_A_log': ssm_A_log, 'ssm_D': ssm_D, 'ssm_norm_g': ssm_norm_g,
            'rwkv_mu': rwkv_mu, 'rwkv_w0': rwkv_w0, 'rwkv_w2': rwkv_w2, 'rwkv_a0': rwkv_a0,
            'rwkv_a2': rwkv_a2, 'rwkv_k_k': rwkv_k_k, 'rwkv_k_a': rwkv_k_a, 'rwkv_r_k': rwkv_r_k,
            'rwkv_gn_g': rwkv_gn_g, 'rwkv_gn_b': rwkv_gn_b,
            'mlp_ln_g': mlp_ln_g, 'mlp_ln_b': mlp_ln_b, 'mlp_w_s': mlp_w_s, 'mlp_b_s': mlp_b_s,
            'final_norm_g': final_norm_g}


def reference(x, norm_g, w_in, w_out, ssm_conv_w, ssm_conv_b, ssm_dt_bias, ssm_A_log, ssm_D, ssm_norm_g,
              rwkv_mu, rwkv_w0, rwkv_w2, rwkv_a0, rwkv_a2, rwkv_k_k, rwkv_k_a, rwkv_r_k, rwkv_gn_g, rwkv_gn_b,
              mlp_ln_g, mlp_ln_b, mlp_w_s, mlp_b_s, final_norm_g):
    h = x
    for l in range(DEPTH):
        h = hybrid_layer(h, norm_g[l], w_in[l], w_out[l],
                         ssm_conv_w[l], ssm_conv_b[l], ssm_dt_bias[l], ssm_A_log[l], ssm_D[l], ssm_norm_g[l],
                         rwkv_mu[l], rwkv_w0[l], rwkv_w2[l], rwkv_a0[l], rwkv_a2[l], rwkv_k_k[l], rwkv_k_a[l],
                         rwkv_r_k[l], rwkv_gn_g[l], rwkv_gn_b[l],
                         mlp_ln_g[l], mlp_ln_b[l], mlp_w_s[l], mlp_b_s[l])
    return rms_norm(h, final_norm_g)
```

```python
import functools
import math

import numpy as np
import jax
import jax.numpy as jnp
from jax import lax
from jax.experimental import pallas as pl
from jax.experimental.pallas import tpu as pltpu

D_MODEL = 4096
BR_W = 1024
ATT_HEAD_DIM = 128
ATT_HEADS = 8
IDX_HEADS = 16
IDX_DIM = 64
IDX_TOPK_MAX = 256
SSM_HEAD_DIM = 64
SSM_HEADS = 16
SSM_GROUPS = 4
SSM_STATE = 128
SSM_CONV = 4
SSM_CHUNK = 128
RWKV_HEAD_DIM = 64
RWKV_LORA = 64
MLP_GROUPS = 8
MLP_CHUNK = 128
NORM_EPS = 1e-5
GN_EPS = 64e-5

LANES = 128
SUBLANES = 8

C_ATT_Q, C_ATT_K, C_ATT_V, C_ATT_Z, C_ATT_QI = 0, 1024, 2048, 3072, 4096
C_SSM_Z, C_SSM_XBC = 5120, 6144
C_RWKV = 8192
C_MLP_U, C_MLP_V, C_MLP_Z = 12288, 13312, 14336
C_ATT_KW, C_SSM_DT, C_RWKV_LORA = 15360, 15488, 15616
N_PACK = 15872

_S_ATT, _S_SSM, _S_RWKV, _S_MLP = 0, 5200, 8288, 12512

Q_BLK = 128
KEY_TILE = 512
ATT_TQ = 512
NEG_BIG = -1e30
INT_MIN = -2147483648

_VMEM_LIMIT = 56 * 1024 * 1024


def _cparams(sem, vmem=_VMEM_LIMIT):
    return pltpu.CompilerParams(dimension_semantics=sem, vmem_limit_bytes=vmem)


def _sigmoid(x):
    return 1.0 / (1.0 + jnp.exp(-x))


def _silu(x):
    return x * _sigmoid(x)


def _softplus(x):
    return jnp.maximum(x, 0.0) + jnp.log1p(jnp.exp(-jnp.abs(x)))


def _split3(x):
    hi = x.astype(jnp.bfloat16)
    r1 = x - hi.astype(jnp.float32)
    mid = r1.astype(jnp.bfloat16)
    lo = (r1 - mid.astype(jnp.float32)).astype(jnp.bfloat16)
    return hi, mid, lo


def _dot_sel_rhs(x, sel):
    hi, mid, lo = _split3(x)
    f = lambda p: jnp.dot(p, sel, preferred_element_type=jnp.float32)
    return f(hi) + f(mid) + f(lo)


def _dot_sel_lhs(sel, x):
    hi, mid, lo = _split3(x)
    f = lambda p: jnp.dot(sel, p, preferred_element_type=jnp.float32)
    return f(hi) + f(mid) + f(lo)


def _seg_sum_lanes(x, seg_ones):
    parts = [_dot_sel_rhs(x[:, c * LANES:(c + 1) * LANES], seg_ones) for c in range(x.shape[1] // LANES)]
    return parts[0] if len(parts) == 1 else jnp.concatenate(parts, axis=1)


def _rmsnorm_kernel(x_ref, g_ref, o_ref):
    x = x_ref[...]
    ms = jnp.mean(x * x, axis=-1, keepdims=True)
    o_ref[...] = (x * lax.rsqrt(ms + NORM_EPS) * g_ref[...]).astype(o_ref.dtype)


def _rmsnorm(x2d, g, out_dtype, rows=256):
    m, d = x2d.shape
    return pl.pallas_call(
        _rmsnorm_kernel,
        out_shape=jax.ShapeDtypeStruct((m, d), out_dtype),
        grid=(m // rows,),
        in_specs=[pl.BlockSpec((rows, d), lambda i: (i, 0)), pl.BlockSpec((1, d), lambda i: (0, 0))],
        out_specs=pl.BlockSpec((rows, d), lambda i: (i, 0)),
        compiler_params=_cparams(("parallel",)),
        name="rmsnorm",
    )(x2d, g.reshape(1, d))


def _matmul_kernel(a_ref, w_ref, o_ref):
    o_ref[...] = jnp.dot(a_ref[...], w_ref[...], preferred_element_type=jnp.float32)


def _in_proj(h_bf16, w_bf16, tm=1024, tn=512):
    m, k = h_bf16.shape
    n = w_bf16.shape[1]
    tm = min(tm, m)
    return pl.pallas_call(
        _matmul_kernel,
        out_shape=jax.ShapeDtypeStruct((m, n), jnp.float32),
        grid=(m // tm, n // tn),
        in_specs=[pl.BlockSpec((tm, k), lambda i, j: (i, 0)), pl.BlockSpec((k, tn), lambda i, j: (0, j))],
        out_specs=pl.BlockSpec((tm, tn), lambda i, j: (i, j)),
        compiler_params=_cparams(("parallel", "arbitrary")),
        name="in_proj",
    )(h_bf16, w_bf16)


def _out_proj_kernel(x_ref, y0_ref, y1_ref, y2_ref, y3_ref, w_ref, o_ref):
    acc = x_ref[...]
    for g, y_ref in enumerate((y0_ref, y1_ref, y2_ref, y3_ref)):
        acc = acc + jnp.dot(y_ref[...], w_ref[g * BR_W:(g + 1) * BR_W, :], preferred_element_type=jnp.float32)
    o_ref[...] = acc


def _out_proj(x2d, ys, w_bf16, tm=512, tn=1024):
    m, d = x2d.shape
    tm = min(tm, m)
    yspec = pl.BlockSpec((tm, BR_W), lambda i, j: (i, 0))
    return pl.pallas_call(
        _out_proj_kernel,
        out_shape=jax.ShapeDtypeStruct((m, d), jnp.float32),
        grid=(m // tm, d // tn),
        in_specs=[pl.BlockSpec((tm, tn), lambda i, j: (i, j)), yspec, yspec, yspec, yspec,
                  pl.BlockSpec((4 * BR_W, tn), lambda i, j: (0, j))],
        out_specs=pl.BlockSpec((tm, tn), lambda i, j: (i, j)),
        compiler_params=_cparams(("parallel", "arbitrary")),
        name="out_proj",
    )(x2d, *ys, w_bf16)


def _indexer_kernel(qi_ref, kwq_ref, kw_ref, mask_ref, key_ref, *, topk, idx_bits):
    i = pl.program_id(1)
    n_tiles_max = key_ref.shape[0]
    n_tiles = (i * Q_BLK + Q_BLK + KEY_TILE - 1) // KEY_TILE
    idx_scale = IDX_DIM ** -0.5
    head_w_scale = IDX_HEADS ** -0.5

    q = qi_ref[...]
    qh = [q[:, h * IDX_DIM:(h + 1) * IDX_DIM] for h in range(IDX_HEADS)]
    wq = kwq_ref[:, IDX_DIM:IDX_DIM + IDX_HEADS] * head_w_scale
    row = i * Q_BLK + lax.broadcasted_iota(jnp.int32, (Q_BLK, 1), 0)
    col0 = lax.broadcasted_iota(jnp.int32, (1, KEY_TILE), 1)

    def score_tile(kt, carry):
        k0 = pl.multiple_of(kt * KEY_TILE, KEY_TILE)
        kidx = kw_ref[pl.ds(k0, KEY_TILE), :][:, :IDX_DIM]
        acc = jnp.zeros((Q_BLK, KEY_TILE), jnp.float32)
        for h in range(IDX_HEADS):
            d = lax.dot_general(qh[h], kidx, (((1,), (1,)), ((), ())),
                                preferred_element_type=jnp.float32) * idx_scale
            acc = acc + wq[:, h:h + 1] * jnp.maximum(d, 0.0)
        acc = jnp.where(k0 + col0 <= row, acc, -jnp.inf)
        bits = pltpu.bitcast(acc, jnp.int32)
        bits = jnp.where(bits == INT_MIN, 0, bits)
        key_ref[kt] = bits ^ ((bits >> 31) & 0x7FFFFFFF)
        return carry

    lax.fori_loop(0, n_tiles, score_tile, 0)

    def count(pred_fn):
        def body(kt, acc):
            m = pred_fn(key_ref[kt], kt * KEY_TILE + col0)
            ones = jnp.where(m, 1.0, 0.0)
            for c in range(KEY_TILE // LANES):
                acc = acc + ones[:, c * LANES:(c + 1) * LANES]
            return acc
        acc = lax.fori_loop(0, n_tiles, body, jnp.zeros((Q_BLK, LANES), jnp.float32))
        return jnp.sum(acc, axis=1, keepdims=True)

    kf = float(topk)

    def thr_bit(it, res_u):
        bit = lax.shift_left(jnp.int32(1), jnp.int32(31) - it)
        cand_u = res_u | bit
        cand_s = cand_u ^ INT_MIN
        cnt = count(lambda key, col: key >= cand_s)
        return jnp.where(cnt >= kf, cand_u, res_u)

    res_u = lax.fori_loop(0, 32, thr_bit, jnp.zeros((Q_BLK, 1), jnp.int32))
    thr = res_u ^ INT_MIN

    n_gt = count(lambda key, col: key > thr)
    need = kf - n_gt

    def tie_bit(it, res):
        cand = res + lax.shift_left(jnp.int32(1), jnp.int32(idx_bits - 1) - it)
        cnt = count(lambda key, col: (key == thr) & (col < cand))
        return jnp.where(cnt < need, cand, res)

    p_star = lax.fori_loop(0, idx_bits, tie_bit, jnp.zeros((Q_BLK, 1), jnp.int32))

    def write_tile(kt, carry):
        key = key_ref[kt]
        col = kt * KEY_TILE + col0
        sel = (key > thr) | ((key == thr) & (col <= p_star))
        sel = sel & (col <= row)
        mask_ref[kt] = jnp.where(sel, 1.0, 0.0).astype(mask_ref.dtype)
        return carry

    lax.fori_loop(0, n_tiles, write_tile, 0)

    def zero_tile(kt, carry):
        mask_ref[kt] = jnp.zeros((Q_BLK, KEY_TILE), mask_ref.dtype)
        return carry

    lax.fori_loop(n_tiles, n_tiles_max, zero_tile, 0)


def _indexer(proj3, topk):
    b, s, _ = proj3.shape
    nqb, nkt = s // Q_BLK, s // KEY_TILE
    idx_bits = max(1, int(math.ceil(math.log2(s))))
    kern = functools.partial(_indexer_kernel, topk=topk, idx_bits=idx_bits)
    return pl.pallas_call(
        kern,
        out_shape=jax.ShapeDtypeStruct((b, nqb, nkt, Q_BLK, KEY_TILE), jnp.bfloat16),
        grid=(b, nqb),
        in_specs=[
            pl.BlockSpec((None, Q_BLK, BR_W), lambda bi, i: (bi, i, C_ATT_QI // BR_W)),
            pl.BlockSpec((None, Q_BLK, LANES), lambda bi, i: (bi, i, C_ATT_KW // LANES)),
            pl.BlockSpec((None, s, LANES), lambda bi, i: (bi, 0, C_ATT_KW // LANES)),
        ],
        out_specs=pl.BlockSpec((None, None, nkt, Q_BLK, KEY_TILE), lambda bi, i: (bi, i, 0, 0, 0)),
        scratch_shapes=[pltpu.VMEM((nkt, Q_BLK, KEY_TILE), jnp.int32)],
        compiler_params=_cparams(("parallel", "arbitrary")),
        name="dsa_indexer",
    )(proj3, proj3, proj3)


def _attn_kernel(q_ref, k_ref, v_ref, m_ref, z_ref, o_ref, m_scr, l_scr, acc_scr):
    qi, ki = pl.program_id(1), pl.program_id(2)
    nk = pl.num_programs(2)
    scale = ATT_HEAD_DIM ** -0.5

    @pl.when(ki == 0)
    def _():
        m_scr[...] = jnp.full(m_scr.shape, NEG_BIG, jnp.float32)
        l_scr[...] = jnp.zeros(l_scr.shape, jnp.float32)
        acc_scr[...] = jnp.zeros(acc_scr.shape, jnp.float32)

    @pl.when(ki <= qi)
    def _():
        tq = q_ref.shape[0]
        sel = m_ref[...].reshape(tq, KEY_TILE) > 0
        for h in range(ATT_HEADS):
            hs = slice(h * ATT_HEAD_DIM, (h + 1) * ATT_HEAD_DIM)
            s = lax.dot_general(q_ref[:, hs], k_ref[:, hs], (((1,), (1,)), ((), ())),
                                preferred_element_type=jnp.float32) * scale
            s = jnp.where(sel, s, NEG_BIG)
            m_prev = m_scr[h]
            m_new = jnp.maximum(m_prev, jnp.max(s, axis=1, keepdims=True))
            alpha = jnp.exp(m_prev - m_new)
            p = jnp.where(sel, jnp.exp(s - m_new[:, :1]), 0.0)
            l_scr[h] = alpha * l_scr[h] + jnp.sum(p, axis=1, keepdims=True)
            acc_scr[:, hs] = acc_scr[:, hs] * alpha + jnp.dot(p, v_ref[:, hs], preferred_element_type=jnp.float32)
            m_scr[h] = m_new

    @pl.when(ki == nk - 1)
    def _():
        z = z_ref[...]
        outs = []
        for h in range(ATT_HEADS):
            hs = slice(h * ATT_HEAD_DIM, (h + 1) * ATT_HEAD_DIM)
            outs.append(acc_scr[:, hs] / l_scr[h])
        o_ref[...] = (jnp.concatenate(outs, axis=1) * _silu(z)).astype(o_ref.dtype)


def _attention(proj3, mask5):
    b, s, _ = proj3.shape
    tq = min(ATT_TQ, s)
    nq, nk = s // tq, s // KEY_TILE
    qpb = tq // Q_BLK
    kv_spec = lambda c: pl.BlockSpec((None, KEY_TILE, BR_W), lambda bi, qi, ki: (bi, jnp.minimum(ki, qi), c // BR_W))
    return pl.pallas_call(
        _attn_kernel,
        out_shape=jax.ShapeDtypeStruct((b, s, BR_W), jnp.bfloat16),
        grid=(b, nq, nk),
        in_specs=[
            pl.BlockSpec((None, tq, BR_W), lambda bi, qi, ki: (bi, qi, C_ATT_Q // BR_W)),
            kv_spec(C_ATT_K), kv_spec(C_ATT_V),
            pl.BlockSpec((None, qpb, None, Q_BLK, KEY_TILE), lambda bi, qi, ki: (bi, qi, jnp.minimum(ki, qi), 0, 0)),
            pl.BlockSpec((None, tq, BR_W), lambda bi, qi, ki: (bi, qi, C_ATT_Z // BR_W)),
        ],
        out_specs=pl.BlockSpec((None, tq, BR_W), lambda bi, qi, ki: (bi, qi, 0)),
        scratch_shapes=[pltpu.VMEM((ATT_HEADS, tq, LANES), jnp.float32),
                        pltpu.VMEM((ATT_HEADS, tq, LANES), jnp.float32),
                        pltpu.VMEM((tq, BR_W), jnp.float32)],
        compiler_params=_cparams(("parallel", "parallel", "arbitrary")),
        name="dsa_attention",
    )(proj3, proj3, proj3, mask5, proj3)


def _ssm_kernel(z_ref, xbc_ref, dt_ref, cw_ref, cb_ref, dtb_ref, alog_ref, dskip_ref, ng_ref,
                expand_ref, tri_ref, o_ref, ext_scr, state_scr):
    c = pl.program_id(1)
    L = SSM_CHUNK
    hpg = SSM_HEADS // SSM_GROUPS
    gw = hpg * SSM_HEAD_DIM

    @pl.when(c == 0)
    def _():
        ext_scr[0:SUBLANES, :] = jnp.zeros((SUBLANES, ext_scr.shape[1]), jnp.float32)
        state_scr[...] = jnp.zeros(state_scr.shape, jnp.float32)

    ext_scr[SUBLANES:SUBLANES + L, :] = xbc_ref[...]
    conv = cb_ref[...] + cw_ref[0:1, :] * ext_scr[pl.ds(SUBLANES - 3, L), :]
    for i in range(1, SSM_CONV):
        conv = conv + cw_ref[i:i + 1, :] * ext_scr[pl.ds(SUBLANES - 3 + i, L), :]
    ext_scr[0:SUBLANES, :] = ext_scr[L:L + SUBLANES, :]
    xbc = _silu(conv)
    xs = xbc[:, :BR_W]
    bm = xbc[:, BR_W:BR_W + SSM_GROUPS * SSM_STATE]
    cm = xbc[:, BR_W + SSM_GROUPS * SSM_STATE:]

    dt = _softplus(dt_ref[...] + dtb_ref[...])
    a = dt * (-jnp.exp(alog_ref[...]))
    tri = tri_ref[...]
    expand = expand_ref[...]
    a_cum = _dot_sel_lhs(tri, a)
    a_cum_t = a_cum.T
    a_cum_e = _dot_sel_rhs(a_cum, expand)
    dt_e = _dot_sel_rhs(dt, expand)
    a_last_e = a_cum_e[L - 1:L, :]
    xdt = xs * dt_e
    xw = xdt * jnp.exp(a_last_e - a_cum_e)

    rows = lax.broadcasted_iota(jnp.int32, (L, L), 0)
    cols = lax.broadcasted_iota(jnp.int32, (L, L), 1)
    lower = rows >= cols

    y_diag_parts = []
    y_off_parts = []
    new_state_parts = []
    for g in range(SSM_GROUPS):
        bg = bm[:, g * SSM_STATE:(g + 1) * SSM_STATE]
        cg = cm[:, g * SSM_STATE:(g + 1) * SSM_STATE]
        cb = lax.dot_general(cg, bg, (((1,), (1,)), ((), ())), preferred_element_type=jnp.float32)
        y_off_parts.append(jnp.dot(cg, state_scr[:, g * gw:(g + 1) * gw], preferred_element_type=jnp.float32))
        for hh in range(hpg):
            h = g * hpg + hh
            seg = a_cum[:, h:h + 1] - a_cum_t[h:h + 1, :]
            w = jnp.where(lower, cb * jnp.exp(jnp.where(lower, seg, 0.0)), 0.0)
            xh = xdt[:, h * SSM_HEAD_DIM:(h + 1) * SSM_HEAD_DIM]
            y_diag_parts.append(jnp.dot(w, xh, preferred_element_type=jnp.float32))
        new_state_parts.append(jnp.dot(bg.T, xw[:, g * gw:(g + 1) * gw], preferred_element_type=jnp.float32))
    y = jnp.concatenate(y_diag_parts, axis=1) + jnp.concatenate(y_off_parts, axis=1) * jnp.exp(a_cum_e)
    state_scr[...] = state_scr[...] * jnp.exp(a_last_e) + jnp.concatenate(new_state_parts, axis=1)

    y = y + dskip_ref[...] * xs
    y = y * _silu(z_ref[...])
    outs = []
    for g in range(SSM_GROUPS):
        yg = y[:, g * gw:(g + 1) * gw]
        outs.append(yg * lax.rsqrt(jnp.mean(yg * yg, axis=-1, keepdims=True) + NORM_EPS))
    o_ref[...] = (jnp.concatenate(outs, axis=1) * ng_ref[...]).astype(o_ref.dtype)


def _ssm(proj3, conv_w, conv_b, dt_bias, a_log, d_skip, norm_g):
    b, s, _ = proj3.shape
    L = SSM_CHUNK
    cch = conv_w.shape[1]
    pad16 = lambda v: jnp.pad(v.reshape(1, SSM_HEADS), ((0, 0), (0, LANES - SSM_HEADS)))
    expand = np.zeros((LANES, BR_W), np.float32)
    for h in range(SSM_HEADS):
        expand[h, h * SSM_HEAD_DIM:(h + 1) * SSM_HEAD_DIM] = 1.0
    tri = np.tril(np.ones((L, L), np.float32))
    full = lambda shape: pl.BlockSpec(shape, lambda bi, c: (0,) * len(shape))
    return pl.pallas_call(
        _ssm_kernel,
        out_shape=jax.ShapeDtypeStruct((b, s, BR_W), jnp.bfloat16),
        grid=(b, s // L),
        in_specs=[
            pl.BlockSpec((None, L, BR_W), lambda bi, c: (bi, c, C_SSM_Z // BR_W)),
            pl.BlockSpec((None, L, cch), lambda bi, c: (bi, c, C_SSM_XBC // cch)),
            pl.BlockSpec((None, L, LANES), lambda bi, c: (bi, c, C_SSM_DT // LANES)),
            full((SSM_CONV, cch)), full((1, cch)), full((1, LANES)), full((1, LANES)),
            full((1, BR_W)), full((1, BR_W)), full((LANES, BR_W)), full((L, L)),
        ],
        out_specs=pl.BlockSpec((None, L, BR_W), lambda bi, c: (bi, c, 0)),
        scratch_shapes=[pltpu.VMEM((L + 2 * SUBLANES, cch), jnp.float32),
                        pltpu.VMEM((SSM_STATE, BR_W), jnp.float32)],
        compiler_params=_cparams(("parallel", "arbitrary")),
        name="ssd",
    )(proj3, proj3, proj3, conv_w, conv_b.reshape(1, cch), pad16(dt_bias), pad16(a_log),
      jnp.repeat(d_skip, SSM_HEAD_DIM).reshape(1, BR_W), norm_g.reshape(1, BR_W),
      jnp.asarray(expand, jnp.bfloat16), jnp.asarray(tri, jnp.bfloat16))


def _rwkv_prep_kernel(main_ref, lora_ref, mu_m_ref, mu_l_ref, w0_ref, w2_ref, a0_ref, a2_ref,
                      kk_ref, ka_ref, rk_ref, seg_ref,
                      r_o, w_o, k_o, v_o, a_o, b_o, bonus_o, sz_o, carry_m, carry_l):
    t = pl.program_id(1)
    rows = main_ref.shape[0]

    @pl.when(t == 0)
    def _():
        carry_m[...] = jnp.zeros(carry_m.shape, jnp.float32)
        carry_l[...] = jnp.zeros(carry_l.shape, jnp.float32)

    first = lax.broadcasted_iota(jnp.int32, (rows, 1), 0) == 0

    def shift_mix(cur, carry_ref, mu):
        prev = jnp.where(first, carry_ref[0:1, :], pltpu.roll(cur, 1, axis=0))
        carry_ref[0:1, :] = cur[rows - 1:rows, :]
        return cur + (prev - cur) * mu

    pm = shift_mix(main_ref[...], carry_m, mu_m_ref[...])
    pll = shift_mix(lora_ref[...], carry_l, mu_l_ref[...])
    r = pm[:, 0:BR_W]
    k = pm[:, BR_W:2 * BR_W]
    v = pm[:, 2 * BR_W:3 * BR_W]
    z = pm[:, 3 * BR_W:4 * BR_W]
    wl = pll[:, :RWKV_LORA]
    al = pll[:, RWKV_LORA:]
    w_raw = -_softplus(-(w0_ref[...] + jnp.dot(jnp.tanh(wl), w2_ref[...], preferred_element_type=jnp.float32))) - 0.5
    decay = jnp.exp(-jnp.exp(w_raw))
    a_sig = _sigmoid(a0_ref[...] + jnp.dot(al, a2_ref[...], preferred_element_type=jnp.float32))
    kk = k * kk_ref[...]
    k_mod = k * (1.0 + (a_sig - 1.0) * ka_ref[...])
    seg = seg_ref[...]
    kk = kk * lax.rsqrt(_seg_sum_lanes(kk * kk, seg) + 1e-12)
    r_o[...] = r
    w_o[...] = decay
    k_o[...] = k_mod
    v_o[...] = v
    a_o[...] = -kk
    b_o[...] = kk * a_sig
    bonus_o[...] = _seg_sum_lanes(r * k_mod * rk_ref[...], seg) * v
    sz_o[...] = _silu(z)


def _seg_ones():
    m = np.zeros((LANES, LANES), np.float32)
    m[:RWKV_HEAD_DIM, :RWKV_HEAD_DIM] = 1.0
    m[RWKV_HEAD_DIM:, RWKV_HEAD_DIM:] = 1.0
    return jnp.asarray(m, jnp.bfloat16)


def _rwkv_prep(proj3, mu_main, mu_lora, w0, w2, a0, a2, k_k, k_a, r_k, rows=256):
    b, s, _ = proj3.shape
    rows = min(rows, s)
    row1 = lambda v: v.reshape(1, -1)
    full = lambda shape: pl.BlockSpec(shape, lambda bi, t: (0,) * len(shape))
    o_spec = pl.BlockSpec((None, rows, BR_W), lambda bi, t: (bi, t, 0))
    o_shape = jax.ShapeDtypeStruct((b, s, BR_W), jnp.float32)
    return pl.pallas_call(
        _rwkv_prep_kernel,
        out_shape=(o_shape,) * 8,
        grid=(b, s // rows),
        in_specs=[
            pl.BlockSpec((None, rows, 4 * BR_W), lambda bi, t: (bi, t, C_RWKV // (4 * BR_W))),
            pl.BlockSpec((None, rows, LANES), lambda bi, t: (bi, t, C_RWKV_LORA // LANES)),
            full((1, 4 * BR_W)), full((1, LANES)), full((1, BR_W)), full((RWKV_LORA, BR_W)),
            full((1, BR_W)), full((RWKV_LORA, BR_W)), full((1, BR_W)), full((1, BR_W)), full((1, BR_W)),
            full((LANES, LANES)),
        ],
        out_specs=(o_spec,) * 8,
        scratch_shapes=[pltpu.VMEM((SUBLANES, 4 * BR_W), jnp.float32), pltpu.VMEM((SUBLANES, LANES), jnp.float32)],
        compiler_params=_cparams(("parallel", "arbitrary")),
        name="rwkv_prep",
    )(proj3, proj3, row1(mu_main), row1(mu_lora), row1(w0), w2, row1(a0), a2, row1(k_k), row1(k_a),
      row1(r_k), _seg_ones())


def _rwkv_scan_kernel(r_ref, w_ref, k_ref, v_ref, a_ref, b_ref, onehot_ref, seg_ref, y_ref, st_ref):
    nb, steps = r_ref.shape[0], r_ref.shape[1]
    hd = RWKV_HEAD_DIM

    @pl.when(pl.program_id(0) == 0)
    def _():
        st_ref[...] = jnp.zeros(st_ref.shape, jnp.float32)

    onehot = onehot_ref[...]
    seg = seg_ref[...]

    def seg3(x):
        return _dot_sel_rhs(x.reshape(hd * SUBLANES, LANES), seg).reshape(hd, SUBLANES, LANES)

    def step(t, carry):
        for bi in range(nb):
            st = st_ref[bi]
            sa = seg3(st * a_ref[bi, t][None])
            vb = seg3(v_ref[bi, t][None] * onehot)
            st = st * w_ref[bi, t][None] + sa * b_ref[bi, t][None] + vb * k_ref[bi, t][None]
            st_ref[bi] = st
            red = seg3(st * r_ref[bi, t][None])
            y_ref[bi, t] = jnp.sum(red * onehot, axis=0)
        return carry

    lax.fori_loop(0, steps, step, 0)


def _rwkv_scan(r, w, k, v, a, bb, steps=128):
    b, s, _ = r.shape
    steps = min(steps, s)
    hp = BR_W // LANES
    shp = lambda x: x.reshape(b, s, hp, LANES)
    onehot = np.zeros((RWKV_HEAD_DIM, 1, LANES), np.float32)
    for i in range(RWKV_HEAD_DIM):
        onehot[i, 0, i] = 1.0
        onehot[i, 0, RWKV_HEAD_DIM + i] = 1.0
    io_spec = pl.BlockSpec((b, steps, hp, LANES), lambda c: (0, c, 0, 0))
    y = pl.pallas_call(
        _rwkv_scan_kernel,
        out_shape=jax.ShapeDtypeStruct((b, s, hp, LANES), jnp.float32),
        grid=(s // steps,),
        in_specs=[io_spec] * 6 + [pl.BlockSpec((RWKV_HEAD_DIM, 1, LANES), lambda c: (0, 0, 0)),
                                  pl.BlockSpec((LANES, LANES), lambda c: (0, 0))],
        out_specs=io_spec,
        scratch_shapes=[pltpu.VMEM((b, RWKV_HEAD_DIM, hp, LANES), jnp.float32)],
        compiler_params=_cparams(("arbitrary",)),
        name="rwkv_scan",
    )(shp(r), shp(w), shp(k), shp(v), shp(a), shp(bb), jnp.asarray(onehot), _seg_ones())
    return y.reshape(b, s, BR_W)


def _rwkv_post_kernel(y_ref, bonus_ref, sz_ref, g_ref, b_ref, seg_ref, o_ref):
    y = y_ref[...]
    seg = seg_ref[...]
    inv = 1.0 / RWKV_HEAD_DIM
    mean = _seg_sum_lanes(y, seg) * inv
    d = y - mean
    var = _seg_sum_lanes(d * d, seg) * inv
    yn = d * lax.rsqrt(var + GN_EPS) * g_ref[...] + b_ref[...]
    o_ref[...] = ((yn + bonus_ref[...]) * sz_ref[...]).astype(o_ref.dtype)


def _rwkv_post(y, bonus, sz, gn_g, gn_b, rows=256):
    b, s, _ = y.shape
    rows = min(rows, s)
    spec = pl.BlockSpec((None, rows, BR_W), lambda bi, t: (bi, t, 0))
    full = lambda shape: pl.BlockSpec(shape, lambda bi, t: (0,) * len(shape))
    return pl.pallas_call(
        _rwkv_post_kernel,
        out_shape=jax.ShapeDtypeStruct((b, s, BR_W), jnp.bfloat16),
        grid=(b, s // rows),
        in_specs=[spec, spec, spec, full((1, BR_W)), full((1, BR_W)), full((LANES, LANES))],
        out_specs=spec,
        compiler_params=_cparams(("parallel", "parallel")),
        name="rwkv_post",
    )(y, bonus, sz, gn_g.reshape(1, BR_W), gn_b.reshape(1, BR_W), _seg_ones())


def _mlp_kernel(u_ref, v_ref, z_ref, g_ref, b_ref, ws_ref, bias_ref, o_ref):
    T = MLP_CHUNK
    gw = BR_W // MLP_GROUPS
    v = v_ref[...]
    mean = jnp.mean(v, axis=-1, keepdims=True)
    d = v - mean
    var = jnp.mean(d * d, axis=-1, keepdims=True)
    vn = d * lax.rsqrt(var + NORM_EPS) * g_ref[...] + b_ref[...]
    lower = lax.broadcasted_iota(jnp.int32, (T, T), 0) >= lax.broadcasted_iota(jnp.int32, (T, T), 1)
    parts = []
    for g in range(MLP_GROUPS):
        wc = jnp.where(lower, ws_ref[g], 0.0)
        parts.append(jnp.dot(wc, vn[:, g * gw:(g + 1) * gw], preferred_element_type=jnp.float32))
    vm = jnp.concatenate(parts, axis=1) + bias_ref[...]
    o_ref[...] = (u_ref[...] * vm * _silu(z_ref[...])).astype(o_ref.dtype)


def _mlp(proj3, ln_g, ln_b, w_s, b_s):
    b, s, _ = proj3.shape
    T = MLP_CHUNK
    gw = BR_W // MLP_GROUPS
    bias = jnp.repeat(jnp.transpose(b_s), gw, axis=1)
    spec = lambda c: pl.BlockSpec((None, T, BR_W), lambda bi, t: (bi, t, c // BR_W))
    full = lambda shape: pl.BlockSpec(shape, lambda bi, t: (0,) * len(shape))
    return pl.pallas_call(
        _mlp_kernel,
        out_shape=jax.ShapeDtypeStruct((b, s, BR_W), jnp.bfloat16),
        grid=(b, s // T),
        in_specs=[spec(C_MLP_U), spec(C_MLP_V), spec(C_MLP_Z), full((1, BR_W)), full((1, BR_W)),
                  full((MLP_GROUPS, T, T)), full((T, BR_W))],
        out_specs=pl.BlockSpec((None, T, BR_W), lambda bi, t: (bi, t, 0)),
        compiler_params=_cparams(("parallel", "parallel")),
        name="gmlp",
    )(proj3, proj3, proj3, ln_g.reshape(1, BR_W), ln_b.reshape(1, BR_W), w_s, bias)


def _pack_w_in(w):
    d = w.shape[0]
    zeros = lambda n: jnp.zeros((d, n), w.dtype)
    sl = lambda s, n: w[:, s:s + n]
    pieces = [
        sl(_S_ATT, 5 * BR_W),
        sl(_S_SSM, BR_W), sl(_S_SSM + BR_W, 2 * BR_W),
        sl(_S_RWKV, 4 * BR_W),
        sl(_S_MLP, 3 * BR_W),
        sl(_S_ATT + 5 * BR_W, IDX_DIM + IDX_HEADS), zeros(LANES - IDX_DIM - IDX_HEADS),
        sl(_S_SSM + 3 * BR_W, SSM_HEADS), zeros(LANES - SSM_HEADS),
        sl(_S_RWKV + 4 * BR_W, 2 * RWKV_LORA),
        zeros(N_PACK - (C_RWKV_LORA + LANES)),
    ]
    return jnp.concatenate(pieces, axis=1).astype(jnp.bfloat16)


def _layer(x2d, bsz, seq, norm_g, w_in, w_out, conv_w, conv_b, dt_bias, a_log, d_skip, ssm_norm_g,
           mu, w0, w2, a0, a2, k_k, k_a, r_k, gn_g, gn_b, ln_g, ln_b, w_s, b_s):
    h = _rmsnorm(x2d, norm_g, jnp.bfloat16)
    proj = _in_proj(h, _pack_w_in(w_in)).reshape(bsz, seq, N_PACK)
    topk = min(IDX_TOPK_MAX, seq // 4)
    y_att = _attention(proj, _indexer(proj, topk))
    y_ssm = _ssm(proj, conv_w, conv_b, dt_bias, a_log, d_skip, ssm_norm_g)
    r, w, k, v, a, bb, bonus, sz = _rwkv_prep(proj, mu[:4 * BR_W], mu[4 * BR_W:], w0, w2, a0, a2, k_k, k_a,
                                              r_k.reshape(-1))
    y_rwkv = _rwkv_post(_rwkv_scan(r, w, k, v, a, bb), bonus, sz, gn_g, gn_b)
    y_mlp = _mlp(proj, ln_g, ln_b, w_s, b_s)
    flat = lambda t: t.reshape(bsz * seq, BR_W)
    return _out_proj(x2d, [flat(y_att), flat(y_ssm), flat(y_rwkv), flat(y_mlp)], w_out.astype(jnp.bfloat16))


def kernel(x, norm_g, w_in, w_out, ssm_conv_w, ssm_conv_b, ssm_dt_bias, ssm_A_log, ssm_D, ssm_norm_g, rwkv_mu,
           rwkv_w0, rwkv_w2, rwkv_a0, rwkv_a2, rwkv_k_k, rwkv_k_a, rwkv_r_k, rwkv_gn_g, rwkv_gn_b,
           mlp_ln_g, mlp_ln_b, mlp_w_s, mlp_b_s, final_norm_g):
    bsz, seq, d = x.shape
    assert d == D_MODEL and seq % KEY_TILE == 0
    h = x.reshape(bsz * seq, d)
    for l in range(norm_g.shape[0]):
        h = _layer(h, bsz, seq, norm_g[l], w_in[l], w_out[l], ssm_conv_w[l], ssm_conv_b[l], ssm_dt_bias[l],
                   ssm_A_log[l], ssm_D[l], ssm_norm_g[l], rwkv_mu[l], rwkv_w0[l], rwkv_w2[l], rwkv_a0[l],
                   rwkv_a2[l], rwkv_k_k[l], rwkv_k_a[l], rwkv_r_k[l], rwkv_gn_g[l], rwkv_gn_b[l],
                   mlp_ln_g[l], mlp_ln_b[l], mlp_w_s[l], mlp_b_s[l])
    return _rmsnorm(h, final_norm_g, jnp.float32).reshape(bsz, seq, d)
```

```python
import functools
import math

import numpy as np
import jax
import jax.numpy as jnp
from jax import lax
from jax.experimental import pallas as pl
from jax.experimental.pallas import tpu as pltpu

D_MODEL = 4096
BR_W = 1024
ATT_HEAD_DIM = 128
ATT_HEADS = 8
IDX_HEADS = 16
IDX_DIM = 64
IDX_TOPK_MAX = 256
SSM_HEAD_DIM = 64
SSM_HEADS = 16
SSM_GROUPS = 4
SSM_STATE = 128
SSM_CONV = 4
SSM_CHUNK = 128
RWKV_HEAD_DIM = 64
RWKV_LORA = 64
MLP_GROUPS = 8
MLP_CHUNK = 128
NORM_EPS = 1e-5
GN_EPS = 64e-5

LANES = 128
SUBLANES = 8

C_ATT_Q, C_ATT_K, C_ATT_V, C_ATT_Z, C_ATT_QI = 0, 1024, 2048, 3072, 4096
C_SSM_Z, C_SSM_XBC = 5120, 6144
C_RWKV = 8192
C_MLP_U, C_MLP_V, C_MLP_Z = 12288, 13312, 14336
C_ATT_KW, C_SSM_DT, C_RWKV_LORA = 15360, 15488, 15616
N_PACK = 15872

_S_ATT, _S_SSM, _S_RWKV, _S_MLP = 0, 5200, 8288, 12512

Q_BLK = 128
KEY_TILE = 512
ATT_TQ = 512
NEG_BIG = -1e30
INT_MIN = -2147483648

_VMEM_LIMIT = 56 * 1024 * 1024


def _cparams(sem, vmem=_VMEM_LIMIT):
    return pltpu.CompilerParams(dimension_semantics=sem, vmem_limit_bytes=vmem)


def _sigmoid(x):
    return 1.0 / (1.0 + jnp.exp(-x))


def _silu(x):
    return x * _sigmoid(x)


def _softplus(x):
    return jnp.maximum(x, 0.0) + jnp.log1p(jnp.exp(-jnp.abs(x)))


def _split3(x):
    hi = x.astype(jnp.bfloat16)
    r1 = x - hi.astype(jnp.float32)
    mid = r1.astype(jnp.bfloat16)
    lo = (r1 - mid.astype(jnp.float32)).astype(jnp.bfloat16)
    return hi, mid, lo


def _dot_sel_rhs(x, sel):
    hi, mid, lo = _split3(x)
    f = lambda p: jnp.dot(p, sel, preferred_element_type=jnp.float32)
    return f(hi) + f(mid) + f(lo)


def _dot_sel_lhs(sel, x):
    hi, mid, lo = _split3(x)
    f = lambda p: jnp.dot(sel, p, preferred_element_type=jnp.float32)
    return f(hi) + f(mid) + f(lo)


def _seg_sum_lanes(x, seg_ones):
    parts = [_dot_sel_rhs(x[:, c * LANES:(c + 1) * LANES], seg_ones) for c in range(x.shape[1] // LANES)]
    return parts[0] if len(parts) == 1 else jnp.concatenate(parts, axis=1)


def _rmsnorm_kernel(x_ref, g_ref, o_ref):
    x = x_ref[...]
    ms = jnp.mean(x * x, axis=-1, keepdims=True)
    o_ref[...] = (x * lax.rsqrt(ms + NORM_EPS) * g_ref[...]).astype(o_ref.dtype)


def _rmsnorm(x2d, g, out_dtype, rows=256):
    m, d = x2d.shape
    return pl.pallas_call(
        _rmsnorm_kernel,
        out_shape=jax.ShapeDtypeStruct((m, d), out_dtype),
        grid=(m // rows,),
        in_specs=[pl.BlockSpec((rows, d), lambda i: (i, 0)), pl.BlockSpec((1, d), lambda i: (0, 0))],
        out_specs=pl.BlockSpec((rows, d), lambda i: (i, 0)),
        compiler_params=_cparams(("parallel",)),
        name="rmsnorm",
    )(x2d, g.reshape(1, d))


def _matmul_kernel(a_ref, w_ref, o_ref):
    o_ref[...] = jnp.dot(a_ref[...], w_ref[...], preferred_element_type=jnp.float32)


def _in_proj(h_bf16, w_bf16, tm=1024, tn=512):
    m, k = h_bf16.shape
    n = w_bf16.shape[1]
    tm = min(tm, m)
    return pl.pallas_call(
        _matmul_kernel,
        out_shape=jax.ShapeDtypeStruct((m, n), jnp.float32),
        grid=(m // tm, n // tn),
        in_specs=[pl.BlockSpec((tm, k), lambda i, j: (i, 0)), pl.BlockSpec((k, tn), lambda i, j: (0, j))],
        out_specs=pl.BlockSpec((tm, tn), lambda i, j: (i, j)),
        compiler_params=_cparams(("parallel", "arbitrary")),
        name="in_proj",
    )(h_bf16, w_bf16)


def _out_proj_kernel(x_ref, y0_ref, y1_ref, y2_ref, y3_ref, w_ref, o_ref):
    acc = x_ref[...]
    for g, y_ref in enumerate((y0_ref, y1_ref, y2_ref, y3_ref)):
        acc = acc + jnp.dot(y_ref[...], w_ref[g * BR_W:(g + 1) * BR_W, :], preferred_element_type=jnp.float32)
    o_ref[...] = acc


def _out_proj(x2d, ys, w_bf16, tm=512, tn=1024):
    m, d = x2d.shape
    tm = min(tm, m)
    yspec = pl.BlockSpec((tm, BR_W), lambda i, j: (i, 0))
    return pl.pallas_call(
        _out_proj_kernel,
        out_shape=jax.ShapeDtypeStruct((m, d), jnp.float32),
        grid=(m // tm, d // tn),
        in_specs=[pl.BlockSpec((tm, tn), lambda i, j: (i, j)), yspec, yspec, yspec, yspec,
                  pl.BlockSpec((4 * BR_W, tn), lambda i, j: (0, j))],
        out_specs=pl.BlockSpec((tm, tn), lambda i, j: (i, j)),
        compiler_params=_cparams(("parallel", "arbitrary")),
        name="out_proj",
    )(x2d, *ys, w_bf16)


def _indexer_kernel(qi_ref, kwq_ref, kw_ref, mask_ref, key_ref, *, topk, idx_bits):
    i = pl.program_id(1)
    n_tiles_max = key_ref.shape[0]
    n_tiles = (i * Q_BLK + Q_BLK + KEY_TILE - 1) // KEY_TILE
    idx_scale = IDX_DIM ** -0.5
    head_w_scale = IDX_HEADS ** -0.5

    q = qi_ref[...]
    qh = [q[:, h * IDX_DIM:(h + 1) * IDX_DIM] for h in range(IDX_HEADS)]
    wq = kwq_ref[:, IDX_DIM:IDX_DIM + IDX_HEADS] * (head_w_scale * idx_scale)
    row = i * Q_BLK + lax.broadcasted_iota(jnp.int32, (Q_BLK, 1), 0)
    col0 = lax.broadcasted_iota(jnp.int32, (1, KEY_TILE), 1)

    def score_tile(kt, carry):
        k0 = pl.multiple_of(kt * KEY_TILE, KEY_TILE)
        kidx = kw_ref[pl.ds(k0, KEY_TILE), :][:, :IDX_DIM]
        acc = jnp.zeros((Q_BLK, KEY_TILE), jnp.float32)
        for h in range(IDX_HEADS):
            d = lax.dot_general(qh[h], kidx, (((1,), (1,)), ((), ())), preferred_element_type=jnp.float32)
            acc = acc + wq[:, h:h + 1] * jnp.maximum(d, 0.0)
        acc = jnp.where(k0 + col0 <= row, acc, -jnp.inf)
        bits = pltpu.bitcast(acc, jnp.int32)
        bits = jnp.where(bits == INT_MIN, 0, bits)
        key_ref[kt] = bits ^ ((bits >> 31) & 0x7FFFFFFF)
        return carry

    lax.fori_loop(0, n_tiles, score_tile, 0)

    def count(pred_fn):
        def body(kt, acc):
            m = pred_fn(key_ref[kt], kt * KEY_TILE + col0)
            ones = jnp.where(m, 1.0, 0.0)
            for c in range(KEY_TILE // LANES):
                acc = acc + ones[:, c * LANES:(c + 1) * LANES]
            return acc
        acc = lax.fori_loop(0, n_tiles, body, jnp.zeros((Q_BLK, LANES), jnp.float32))
        return jnp.sum(acc, axis=1, keepdims=True)

    kf = float(topk)

    def thr_bit(it, res_u):
        bit = lax.shift_left(jnp.int32(1), jnp.int32(31) - it)
        cand_u = res_u | bit
        cand_s = cand_u ^ INT_MIN
        cnt = count(lambda key, col: key >= cand_s)
        return jnp.where(cnt >= kf, cand_u, res_u)

    res_u = lax.fori_loop(0, 32, thr_bit, jnp.zeros((Q_BLK, 1), jnp.int32))
    thr = res_u ^ INT_MIN

    def tie_break():
        need = kf - count(lambda key, col: key > thr)

        def tie_bit(it, res):
            cand = res + lax.shift_left(jnp.int32(1), jnp.int32(idx_bits - 1) - it)
            cnt = count(lambda key, col: (key == thr) & (col < cand))
            return jnp.where(cnt < need, cand, res)

        return lax.fori_loop(0, idx_bits, tie_bit, jnp.zeros((Q_BLK, 1), jnp.int32))

    n_ge = count(lambda key, col: key >= thr)
    p_star = lax.cond(jnp.max(n_ge) > kf, tie_break,
                      lambda: jnp.full((Q_BLK, 1), n_tiles_max * KEY_TILE, jnp.int32))

    def write_tile(kt, carry):
        key = key_ref[kt]
        col = kt * KEY_TILE + col0
        sel = (key > thr) | ((key == thr) & (col <= p_star))
        sel = sel & (col <= row)
        mask_ref[kt] = jnp.where(sel, 1.0, 0.0).astype(mask_ref.dtype)
        return carry

    lax.fori_loop(0, n_tiles, write_tile, 0)

    def zero_tile(kt, carry):
        mask_ref[kt] = jnp.zeros((Q_BLK, KEY_TILE), mask_ref.dtype)
        return carry

    lax.fori_loop(n_tiles, n_tiles_max, zero_tile, 0)


def _indexer(proj3, topk):
    b, s, _ = proj3.shape
    nqb, nkt = s // Q_BLK, s // KEY_TILE
    idx_bits = max(1, int(math.ceil(math.log2(s))))
    kern = functools.partial(_indexer_kernel, topk=topk, idx_bits=idx_bits)
    return pl.pallas_call(
        kern,
        out_shape=jax.ShapeDtypeStruct((b, nqb, nkt, Q_BLK, KEY_TILE), jnp.bfloat16),
        grid=(b, nqb),
        in_specs=[
            pl.BlockSpec((None, Q_BLK, BR_W), lambda bi, i: (bi, i, C_ATT_QI // BR_W)),
            pl.BlockSpec((None, Q_BLK, LANES), lambda bi, i: (bi, i, C_ATT_KW // LANES)),
            pl.BlockSpec((None, s, LANES), lambda bi, i: (bi, 0, C_ATT_KW // LANES)),
        ],
        out_specs=pl.BlockSpec((None, None, nkt, Q_BLK, KEY_TILE), lambda bi, i: (bi, i, 0, 0, 0)),
        scratch_shapes=[pltpu.VMEM((nkt, Q_BLK, KEY_TILE), jnp.int32)],
        compiler_params=_cparams(("parallel", "arbitrary")),
        name="dsa_indexer",
    )(proj3, proj3, proj3)


def _attn_kernel(q_ref, k_ref, v_ref, m_ref, z_ref, o_ref, m_scr, l_scr, acc_scr):
    qi, ki = pl.program_id(1), pl.program_id(2)
    nk = pl.num_programs(2)
    scale = ATT_HEAD_DIM ** -0.5

    @pl.when(ki == 0)
    def _():
        m_scr[...] = jnp.full(m_scr.shape, NEG_BIG, jnp.float32)
        l_scr[...] = jnp.zeros(l_scr.shape, jnp.float32)
        acc_scr[...] = jnp.zeros(acc_scr.shape, jnp.float32)

    @pl.when(ki <= qi)
    def _():
        tq = q_ref.shape[0]
        sel = m_ref[...].reshape(tq, KEY_TILE) > 0
        for h in range(ATT_HEADS):
            hs = slice(h * ATT_HEAD_DIM, (h + 1) * ATT_HEAD_DIM)
            s = lax.dot_general(q_ref[:, hs], k_ref[:, hs], (((1,), (1,)), ((), ())),
                                preferred_element_type=jnp.float32) * scale
            s = jnp.where(sel, s, NEG_BIG)
            m_prev = m_scr[h]
            m_new = jnp.maximum(m_prev, jnp.max(s, axis=1, keepdims=True))
            alpha = jnp.exp(m_prev - m_new)
            p = jnp.where(sel, jnp.exp(s - m_new[:, :1]), 0.0)
            l_scr[h] = alpha * l_scr[h] + jnp.sum(p, axis=1, keepdims=True)
            acc_scr[:, hs] = acc_scr[:, hs] * alpha + jnp.dot(p, v_ref[:, hs], preferred_element_type=jnp.float32)
            m_scr[h] = m_new

    @pl.when(ki == nk - 1)
    def _():
        z = z_ref[...]
        outs = []
        for h in range(ATT_HEADS):
            hs = slice(h * ATT_HEAD_DIM, (h + 1) * ATT_HEAD_DIM)
            outs.append(acc_scr[:, hs] / l_scr[h])
        o_ref[...] = (jnp.concatenate(outs, axis=1) * _silu(z)).astype(o_ref.dtype)


def _attention(proj3, mask5):
    b, s, _ = proj3.shape
    tq = min(ATT_TQ, s)
    nq, nk = s // tq, s // KEY_TILE
    qpb = tq // Q_BLK
    kv_spec = lambda c: pl.BlockSpec((None, KEY_TILE, BR_W), lambda bi, qi, ki: (bi, jnp.minimum(ki, qi), c // BR_W))
    return pl.pallas_call(
        _attn_kernel,
        out_shape=jax.ShapeDtypeStruct((b, s, BR_W), jnp.bfloat16),
        grid=(b, nq, nk),
        in_specs=[
            pl.BlockSpec((None, tq, BR_W), lambda bi, qi, ki: (bi, qi, C_ATT_Q // BR_W)),
            kv_spec(C_ATT_K), kv_spec(C_ATT_V),
            pl.BlockSpec((None, qpb, None, Q_BLK, KEY_TILE), lambda bi, qi, ki: (bi, qi, jnp.minimum(ki, qi), 0, 0)),
            pl.BlockSpec((None, tq, BR_W), lambda bi, qi, ki: (bi, qi, C_ATT_Z // BR_W)),
        ],
        out_specs=pl.BlockSpec((None, tq, BR_W), lambda bi, qi, ki: (bi, qi, 0)),
        scratch_shapes=[pltpu.VMEM((ATT_HEADS, tq, LANES), jnp.float32),
                        pltpu.VMEM((ATT_HEADS, tq, LANES), jnp.float32),
                        pltpu.VMEM((tq, BR_W), jnp.float32)],
        compiler_params=_cparams(("parallel", "parallel", "arbitrary")),
        name="dsa_attention",
    )(proj3, proj3, proj3, mask5, proj3)


def _ssm_kernel(z_ref, xbc_ref, dt_ref, cw_ref, cb_ref, dtb_ref, alog_ref, dskip_ref, ng_ref,
                expand_ref, tri_ref, o_ref, ext_scr, state_scr):
    c = pl.program_id(1)
    L = SSM_CHUNK
    hpg = SSM_HEADS // SSM_GROUPS
    gw = hpg * SSM_HEAD_DIM

    @pl.when(c == 0)
    def _():
        ext_scr[0:SUBLANES, :] = jnp.zeros((SUBLANES, ext_scr.shape[1]), jnp.float32)
        state_scr[...] = jnp.zeros(state_scr.shape, jnp.float32)

    ext_scr[SUBLANES:SUBLANES + L, :] = xbc_ref[...]
    conv = cb_ref[...] + cw_ref[0:1, :] * ext_scr[pl.ds(SUBLANES - 3, L), :]
    for i in range(1, SSM_CONV):
        conv = conv + cw_ref[i:i + 1, :] * ext_scr[pl.ds(SUBLANES - 3 + i, L), :]
    ext_scr[0:SUBLANES, :] = ext_scr[L:L + SUBLANES, :]
    xbc = _silu(conv)
    xs = xbc[:, :BR_W]
    bm = xbc[:, BR_W:BR_W + SSM_GROUPS * SSM_STATE]
    cm = xbc[:, BR_W + SSM_GROUPS * SSM_STATE:]

    dt = _softplus(dt_ref[...] + dtb_ref[...])
    a = dt * (-jnp.exp(alog_ref[...]))
    tri = tri_ref[...]
    expand = expand_ref[...]
    a_cum = _dot_sel_lhs(tri, a)
    a_cum_t = a_cum.T
    a_cum_e = _dot_sel_rhs(a_cum, expand)
    dt_e = _dot_sel_rhs(dt, expand)
    a_last_e = a_cum_e[L - 1:L, :]
    xdt = xs * dt_e
    xw = xdt * jnp.exp(a_last_e - a_cum_e)

    rows = lax.broadcasted_iota(jnp.int32, (L, L), 0)
    cols = lax.broadcasted_iota(jnp.int32, (L, L), 1)
    lower = rows >= cols

    y_diag_parts = []
    y_off_parts = []
    new_state_parts = []
    for g in range(SSM_GROUPS):
        bg = bm[:, g * SSM_STATE:(g + 1) * SSM_STATE]
        cg = cm[:, g * SSM_STATE:(g + 1) * SSM_STATE]
        cb = lax.dot_general(cg, bg, (((1,), (1,)), ((), ())), preferred_element_type=jnp.float32)
        y_off_parts.append(jnp.dot(cg, state_scr[:, g * gw:(g + 1) * gw], preferred_element_type=jnp.float32))
        for hh in range(hpg):
            h = g * hpg + hh
            seg = a_cum[:, h:h + 1] - a_cum_t[h:h + 1, :]
            w = jnp.where(lower, cb * jnp.exp(jnp.where(lower, seg, 0.0)), 0.0)
            xh = xdt[:, h * SSM_HEAD_DIM:(h + 1) * SSM_HEAD_DIM]
            y_diag_parts.append(jnp.dot(w, xh, preferred_element_type=jnp.float32))
        new_state_parts.append(jnp.dot(bg.T, xw[:, g * gw:(g + 1) * gw], preferred_element_type=jnp.float32))
    y = jnp.concatenate(y_diag_parts, axis=1) + jnp.concatenate(y_off_parts, axis=1) * jnp.exp(a_cum_e)
    state_scr[...] = state_scr[...] * jnp.exp(a_last_e) + jnp.concatenate(new_state_parts, axis=1)

    y = y + dskip_ref[...] * xs
    y = y * _silu(z_ref[...])
    outs = []
    for g in range(SSM_GROUPS):
        yg = y[:, g * gw:(g + 1) * gw]
        outs.append(yg * lax.rsqrt(jnp.mean(yg * yg, axis=-1, keepdims=True) + NORM_EPS))
    o_ref[...] = (jnp.concatenate(outs, axis=1) * ng_ref[...]).astype(o_ref.dtype)


def _ssm(proj3, conv_w, conv_b, dt_bias, a_log, d_skip, norm_g):
    b, s, _ = proj3.shape
    L = SSM_CHUNK
    cch = conv_w.shape[1]
    pad16 = lambda v: jnp.pad(v.reshape(1, SSM_HEADS), ((0, 0), (0, LANES - SSM_HEADS)))
    expand = np.zeros((LANES, BR_W), np.float32)
    for h in range(SSM_HEADS):
        expand[h, h * SSM_HEAD_DIM:(h + 1) * SSM_HEAD_DIM] = 1.0
    tri = np.tril(np.ones((L, L), np.float32))
    full = lambda shape: pl.BlockSpec(shape, lambda bi, c: (0,) * len(shape))
    return pl.pallas_call(
        _ssm_kernel,
        out_shape=jax.ShapeDtypeStruct((b, s, BR_W), jnp.bfloat16),
        grid=(b, s // L),
        in_specs=[
            pl.BlockSpec((None, L, BR_W), lambda bi, c: (bi, c, C_SSM_Z // BR_W)),
            pl.BlockSpec((None, L, cch), lambda bi, c: (bi, c, C_SSM_XBC // cch)),
            pl.BlockSpec((None, L, LANES), lambda bi, c: (bi, c, C_SSM_DT // LANES)),
            full((SSM_CONV, cch)), full((1, cch)), full((1, LANES)), full((1, LANES)),
            full((1, BR_W)), full((1, BR_W)), full((LANES, BR_W)), full((L, L)),
        ],
        out_specs=pl.BlockSpec((None, L, BR_W), lambda bi, c: (bi, c, 0)),
        scratch_shapes=[pltpu.VMEM((L + 2 * SUBLANES, cch), jnp.float32),
                        pltpu.VMEM((SSM_STATE, BR_W), jnp.float32)],
        compiler_params=_cparams(("parallel", "arbitrary")),
        name="ssd",
    )(proj3, proj3, proj3, conv_w, conv_b.reshape(1, cch), pad16(dt_bias), pad16(a_log),
      jnp.repeat(d_skip, SSM_HEAD_DIM).reshape(1, BR_W), norm_g.reshape(1, BR_W),
      jnp.asarray(expand, jnp.bfloat16), jnp.asarray(tri, jnp.bfloat16))


def _rwkv_prep_kernel(main_ref, lora_ref, mu_m_ref, mu_l_ref, w0_ref, w2_ref, a0_ref, a2_ref,
                      kk_ref, ka_ref, rk_ref, seg_ref,
                      r_o, w_o, k_o, v_o, a_o, b_o, bonus_o, sz_o, carry_m, carry_l):
    t = pl.program_id(1)
    rows = main_ref.shape[0]

    @pl.when(t == 0)
    def _():
        carry_m[...] = jnp.zeros(carry_m.shape, jnp.float32)
        carry_l[...] = jnp.zeros(carry_l.shape, jnp.float32)

    first = lax.broadcasted_iota(jnp.int32, (rows, 1), 0) == 0

    def shift_mix(cur, carry_ref, mu):
        prev = jnp.where(first, carry_ref[0:1, :], pltpu.roll(cur, 1, axis=0))
        carry_ref[0:1, :] = cur[rows - 1:rows, :]
        return cur + (prev - cur) * mu

    pm = shift_mix(main_ref[...], carry_m, mu_m_ref[...])
    pll = shift_mix(lora_ref[...], carry_l, mu_l_ref[...])
    r = pm[:, 0:BR_W]
    k = pm[:, BR_W:2 * BR_W]
    v = pm[:, 2 * BR_W:3 * BR_W]
    z = pm[:, 3 * BR_W:4 * BR_W]
    wl = pll[:, :RWKV_LORA]
    al = pll[:, RWKV_LORA:]
    w_raw = -_softplus(-(w0_ref[...] + jnp.dot(jnp.tanh(wl), w2_ref[...], preferred_element_type=jnp.float32))) - 0.5
    decay = jnp.exp(-jnp.exp(w_raw))
    a_sig = _sigmoid(a0_ref[...] + jnp.dot(al, a2_ref[...], preferred_element_type=jnp.float32))
    kk = k * kk_ref[...]
    k_mod = k * (1.0 + (a_sig - 1.0) * ka_ref[...])
    seg = seg_ref[...]
    kk = kk * lax.rsqrt(_seg_sum_lanes(kk * kk, seg) + 1e-12)
    r_o[...] = r
    w_o[...] = decay
    k_o[...] = k_mod
    v_o[...] = v
    a_o[...] = -kk
    b_o[...] = kk * a_sig
    bonus_o[...] = _seg_sum_lanes(r * k_mod * rk_ref[...], seg) * v
    sz_o[...] = _silu(z)


def _seg_ones():
    m = np.zeros((LANES, LANES), np.float32)
    m[:RWKV_HEAD_DIM, :RWKV_HEAD_DIM] = 1.0
    m[RWKV_HEAD_DIM:, RWKV_HEAD_DIM:] = 1.0
    return jnp.asarray(m, jnp.bfloat16)


def _rwkv_prep(proj3, mu_main, mu_lora, w0, w2, a0, a2, k_k, k_a, r_k, rows=256):
    b, s, _ = proj3.shape
    rows = min(rows, s)
    row1 = lambda v: v.reshape(1, -1)
    full = lambda shape: pl.BlockSpec(shape, lambda bi, t: (0,) * len(shape))
    o_spec = pl.BlockSpec((None, rows, BR_W), lambda bi, t: (bi, t, 0))
    o_shape = jax.ShapeDtypeStruct((b, s, BR_W), jnp.float32)
    return pl.pallas_call(
        _rwkv_prep_kernel,
        out_shape=(o_shape,) * 8,
        grid=(b, s // rows),
        in_specs=[
            pl.BlockSpec((None, rows, 4 * BR_W), lambda bi, t: (bi, t, C_RWKV // (4 * BR_W))),
            pl.BlockSpec((None, rows, LANES), lambda bi, t: (bi, t, C_RWKV_LORA // LANES)),
            full((1, 4 * BR_W)), full((1, LANES)), full((1, BR_W)), full((RWKV_LORA, BR_W)),
            full((1, BR_W)), full((RWKV_LORA, BR_W)), full((1, BR_W)), full((1, BR_W)), full((1, BR_W)),
            full((LANES, LANES)),
        ],
        out_specs=(o_spec,) * 8,
        scratch_shapes=[pltpu.VMEM((SUBLANES, 4 * BR_W), jnp.float32), pltpu.VMEM((SUBLANES, LANES), jnp.float32)],
        compiler_params=_cparams(("parallel", "arbitrary")),
        name="rwkv_prep",
    )(proj3, proj3, row1(mu_main), row1(mu_lora), row1(w0), w2, row1(a0), a2, row1(k_k), row1(k_a),
      row1(r_k), _seg_ones())


def _rwkv_scan_kernel(r_ref, w_ref, k_ref, v_ref, a_ref, b_ref, onehot_ref, gidx_ref, seg_ref, y_ref, *st_refs):
    steps = r_ref.shape[1]
    hd = RWKV_HEAD_DIM

    @pl.when(pl.program_id(0) == 0)
    def _():
        for st_ref in st_refs:
            st_ref[...] = jnp.zeros(st_ref.shape, jnp.float32)

    seg = seg_ref[...]

    def seg_sum(x):
        x2 = x.reshape(hd * SUBLANES, LANES).astype(jnp.bfloat16)
        return jnp.dot(x2, seg, preferred_element_type=jnp.float32).reshape(hd, SUBLANES, LANES)

    def value_bcast(vs):
        rows = [[] for _ in vs]
        for i in range(hd):
            idx = gidx_ref[i]
            for bi, v in enumerate(vs):
                rows[bi].append(jnp.take_along_axis(v, idx, axis=1, mode="promise_in_bounds"))
        return [jnp.stack(rw, axis=0) for rw in rows]

    def readout(st, r):
        return jnp.sum(seg_sum(st * r[None]) * onehot_ref[...], axis=0)

    def step(t, carry):
        tp = jnp.maximum(t - 1, 0)
        sts = [st_ref[...] for st_ref in st_refs]
        sas = [seg_sum(st * a_ref[bi, t][None]) for bi, st in enumerate(sts)]
        for bi, st in enumerate(sts):
            y_ref[bi, tp] = readout(st, r_ref[bi, tp])
        vbs = value_bcast([v_ref[bi, t] for bi in range(len(st_refs))])
        for bi, (st_ref, st, sa, vb) in enumerate(zip(st_refs, sts, sas, vbs)):
            st_ref[...] = st * w_ref[bi, t][None] + sa * b_ref[bi, t][None] + vb * k_ref[bi, t][None]
        return carry

    lax.fori_loop(0, steps, step, 0)
    for bi, st_ref in enumerate(st_refs):
        y_ref[bi, steps - 1] = readout(st_ref[...], r_ref[bi, steps - 1])


def _rwkv_scan(r, w, k, v, a, bb, steps=128):
    b, s, _ = r.shape
    steps = min(steps, s)
    hd = RWKV_HEAD_DIM
    hp = BR_W // LANES
    shp = lambda x: x.reshape(b, s, hp, LANES)
    lane = np.arange(LANES)
    onehot = np.broadcast_to(lane[None, None, :] % hd == np.arange(hd)[:, None, None],
                             (hd, SUBLANES, LANES)).astype(np.float32)
    gidx = np.broadcast_to((lane[None, None, :] // hd) * hd + np.arange(hd)[:, None, None],
                           (hd, SUBLANES, LANES)).astype(np.int32)
    io_spec = pl.BlockSpec((b, steps, hp, LANES), lambda c: (0, c, 0, 0))
    const = lambda shape: pl.BlockSpec(shape, lambda c: (0,) * len(shape))
    y = pl.pallas_call(
        _rwkv_scan_kernel,
        out_shape=jax.ShapeDtypeStruct((b, s, hp, LANES), jnp.float32),
        grid=(s // steps,),
        in_specs=[io_spec] * 6 + [const((hd, SUBLANES, LANES)), const((hd, SUBLANES, LANES)),
                                  const((LANES, LANES))],
        out_specs=io_spec,
        scratch_shapes=[pltpu.VMEM((hd, hp, LANES), jnp.float32)] * b,
        compiler_params=_cparams(("arbitrary",)),
        name="rwkv_scan",
    )(shp(r), shp(w), shp(k), shp(v), shp(a), shp(bb), jnp.asarray(onehot), jnp.asarray(gidx),
      _seg_ones())
    return y.reshape(b, s, BR_W)


def _rwkv_post_kernel(y_ref, bonus_ref, sz_ref, g_ref, b_ref, seg_ref, o_ref):
    y = y_ref[...]
    seg = seg_ref[...]
    inv = 1.0 / RWKV_HEAD_DIM
    mean = _seg_sum_lanes(y, seg) * inv
    d = y - mean
    var = _seg_sum_lanes(d * d, seg) * inv
    yn = d * lax.rsqrt(var + GN_EPS) * g_ref[...] + b_ref[...]
    o_ref[...] = ((yn + bonus_ref[...]) * sz_ref[...]).astype(o_ref.dtype)


def _rwkv_post(y, bonus, sz, gn_g, gn_b, rows=256):
    b, s, _ = y.shape
    rows = min(rows, s)
    spec = pl.BlockSpec((None, rows, BR_W), lambda bi, t: (bi, t, 0))
    full = lambda shape: pl.BlockSpec(shape, lambda bi, t: (0,) * len(shape))
    return pl.pallas_call(
        _rwkv_post_kernel,
        out_shape=jax.ShapeDtypeStruct((b, s, BR_W), jnp.bfloat16),
        grid=(b, s // rows),
        in_specs=[spec, spec, spec, full((1, BR_W)), full((1, BR_W)), full((LANES, LANES))],
        out_specs=spec,
        compiler_params=_cparams(("parallel", "parallel")),
        name="rwkv_post",
    )(y, bonus, sz, gn_g.reshape(1, BR_W), gn_b.reshape(1, BR_W), _seg_ones())


def _mlp_kernel(u_ref, v_ref, z_ref, g_ref, b_ref, ws_ref, bias_ref, o_ref):
    T = MLP_CHUNK
    gw = BR_W // MLP_GROUPS
    v = v_ref[...]
    mean = jnp.mean(v, axis=-1, keepdims=True)
    d = v - mean
    var = jnp.mean(d * d, axis=-1, keepdims=True)
    vn = d * lax.rsqrt(var + NORM_EPS) * g_ref[...] + b_ref[...]
    lower = lax.broadcasted_iota(jnp.int32, (T, T), 0) >= lax.broadcasted_iota(jnp.int32, (T, T), 1)
    parts = []
    for g in range(MLP_GROUPS):
        wc = jnp.where(lower, ws_ref[g], 0.0)
        parts.append(jnp.dot(wc, vn[:, g * gw:(g + 1) * gw], preferred_element_type=jnp.float32))
    vm = jnp.concatenate(parts, axis=1) + bias_ref[...]
    o_ref[...] = (u_ref[...] * vm * _silu(z_ref[...])).astype(o_ref.dtype)


def _mlp(proj3, ln_g, ln_b, w_s, b_s):
    b, s, _ = proj3.shape
    T = MLP_CHUNK
    gw = BR_W // MLP_GROUPS
    bias = jnp.repeat(jnp.transpose(b_s), gw, axis=1)
    spec = lambda c: pl.BlockSpec((None, T, BR_W), lambda bi, t: (bi, t, c // BR_W))
    full = lambda shape: pl.BlockSpec(shape, lambda bi, t: (0,) * len(shape))
    return pl.pallas_call(
        _mlp_kernel,
        out_shape=jax.ShapeDtypeStruct((b, s, BR_W), jnp.bfloat16),
        grid=(b, s // T),
        in_specs=[spec(C_MLP_U), spec(C_MLP_V), spec(C_MLP_Z), full((1, BR_W)), full((1, BR_W)),
                  full((MLP_GROUPS, T, T)), full((T, BR_W))],
        out_specs=pl.BlockSpec((None, T, BR_W), lambda bi, t: (bi, t, 0)),
        compiler_params=_cparams(("parallel", "parallel")),
        name="gmlp",
    )(proj3, proj3, proj3, ln_g.reshape(1, BR_W), ln_b.reshape(1, BR_W), w_s, bias)


def _pack_w_in(w):
    d = w.shape[0]
    zeros = lambda n: jnp.zeros((d, n), jnp.bfloat16)
    sl = lambda s, n: w[:, s:s + n].astype(jnp.bfloat16)
    pieces = [
        sl(_S_ATT, 5 * BR_W),
        sl(_S_SSM, BR_W), sl(_S_SSM + BR_W, 2 * BR_W),
        sl(_S_RWKV, 4 * BR_W),
        sl(_S_MLP, 3 * BR_W),
        sl(_S_ATT + 5 * BR_W, IDX_DIM + IDX_HEADS), zeros(LANES - IDX_DIM - IDX_HEADS),
        sl(_S_SSM + 3 * BR_W, SSM_HEADS), zeros(LANES - SSM_HEADS),
        sl(_S_RWKV + 4 * BR_W, 2 * RWKV_LORA),
        zeros(N_PACK - (C_RWKV_LORA + LANES)),
    ]
    return jnp.concatenate(pieces, axis=1)


def _layer(x2d, bsz, seq, norm_g, w_in, w_out, conv_w, conv_b, dt_bias, a_log, d_skip, ssm_norm_g,
           mu, w0, w2, a0, a2, k_k, k_a, r_k, gn_g, gn_b, ln_g, ln_b, w_s, b_s):
    h = _rmsnorm(x2d, norm_g, jnp.bfloat16)
    proj = _in_proj(h, _pack_w_in(w_in)).reshape(bsz, seq, N_PACK)
    topk = min(IDX_TOPK_MAX, seq // 4)
    y_att = _attention(proj, _indexer(proj, topk))
    y_ssm = _ssm(proj, conv_w, conv_b, dt_bias, a_log, d_skip, ssm_norm_g)
    r, w, k, v, a, bb, bonus, sz = _rwkv_prep(proj, mu[:4 * BR_W], mu[4 * BR_W:], w0, w2, a0, a2, k_k, k_a,
                                              r_k.reshape(-1))
    y_rwkv = _rwkv_post(_rwkv_scan(r, w, k, v, a, bb), bonus, sz, gn_g, gn_b)
    y_mlp = _mlp(proj, ln_g, ln_b, w_s, b_s)
    flat = lambda t: t.reshape(bsz * seq, BR_W)
    return _out_proj(x2d, [flat(y_att), flat(y_ssm), flat(y_rwkv), flat(y_mlp)], w_out.astype(jnp.bfloat16))


def kernel(x, norm_g, w_in, w_out, ssm_conv_w, ssm_conv_b, ssm_dt_bias, ssm_A_log, ssm_D, ssm_norm_g, rwkv_mu,
           rwkv_w0, rwkv_w2, rwkv_a0, rwkv_a2, rwkv_k_k, rwkv_k_a, rwkv_r_k, rwkv_gn_g, rwkv_gn_b,
           mlp_ln_g, mlp_ln_b, mlp_w_s, mlp_b_s, final_norm_g):
    bsz, seq, d = x.shape
    assert d == D_MODEL and seq % KEY_TILE == 0
    h = x.reshape(bsz * seq, d)
    for l in range(norm_g.shape[0]):
        h = _layer(h, bsz, seq, norm_g[l], w_in[l], w_out[l], ssm_conv_w[l], ssm_conv_b[l], ssm_dt_bias[l],
                   ssm_A_log[l], ssm_D[l], ssm_norm_g[l], rwkv_mu[l], rwkv_w0[l], rwkv_w2[l], rwkv_a0[l],
                   rwkv_a2[l], rwkv_k_k[l], rwkv_k_a[l], rwkv_r_k[l], rwkv_gn_g[l], rwkv_gn_b[l],
                   mlp_ln_g[l], mlp_ln_b[l], mlp_w_s[l], mlp_b_s[l])
    return _rmsnorm(h, final_norm_g, jnp.float32).reshape(bsz, seq, d)
```
